```python
import jax, jax.numpy as jnp
from jax import lax
import numpy as np

D_MODEL = 2048
BATCH = 1
SEQ = 16384
DEPTH = 1

CHUNK = 64
PLE_DIM = 256
EPS = 1e-6

ATT_HEAD_DIM = 64
ATT_WIDTH = D_MODEL // 2
ATT_HEADS = ATT_WIDTH // ATT_HEAD_DIM
BAND_CHUNKS = 9
MAX_REL_DIST = 256

GLA_HEADS = 4
GLA_WIDTH = D_MODEL // 2
GLA_DV = GLA_WIDTH // GLA_HEADS
GLA_KEY_WIDTH = GLA_WIDTH // 2
GLA_DK = GLA_KEY_WIDTH // GLA_HEADS
GLA_GATE_RANK = 16
GLA_GATE_TAU = 16.0

MIX_WIDTH = ATT_WIDTH + GLA_WIDTH
IN_SPLITS = (ATT_WIDTH, ATT_WIDTH, ATT_WIDTH,
             GLA_KEY_WIDTH, GLA_KEY_WIDTH,
             GLA_WIDTH, GLA_WIDTH,
             GLA_GATE_RANK)
IN_PROJ_WIDTH = sum(IN_SPLITS)

D_FF = 5632
CONV_WIDTH = 3

kernel_name = "hybrid_chunked_attn_gla_convffn_ple"


def rms_norm(x, w):
    xf = x.astype(jnp.float32)
    y = xf * lax.rsqrt(jnp.mean(xf * xf, axis=-1, keepdims=True) + EPS)
    return (y * w.astype(jnp.float32)).astype(x.dtype)


def split_cols(t, sizes):
    idx = [int(v) for v in np.cumsum(sizes)[:-1]]
    return jnp.split(t, idx, axis=-1)


def rel_pos_bias(table):
    q_idx = jnp.arange(CHUNK) + (BAND_CHUNKS - 1) * CHUNK
    k_idx = jnp.arange(BAND_CHUNKS * CHUNK)
    dist = jnp.clip(q_idx[:, None] - k_idx[None, :], -MAX_REL_DIST, MAX_REL_DIST) + MAX_REL_DIST
    return table[:, dist]


def chunked_band_attention(q, k, v, bias_table):
    B, T, H, hd = q.shape
    nc = T // CHUNK
    pad = (BAND_CHUNKS - 1) * CHUNK
    band = BAND_CHUNKS * CHUNK
    k_pad = jnp.pad(k, ((0, 0), (pad, 0), (0, 0), (0, 0)))
    v_pad = jnp.pad(v, ((0, 0), (pad, 0), (0, 0), (0, 0)))
    bias = rel_pos_bias(bias_table).astype(jnp.float32)[None]
    key_off = jnp.arange(band) - pad
    q_chunks = q.reshape(B, nc, CHUNK, H, hd).transpose(1, 0, 2, 3, 4)
    scale = hd ** -0.5
    neg = jnp.finfo(jnp.float32).min

    def one_chunk(args):
        c, qc = args
        start = c * CHUNK
        kb = lax.dynamic_slice_in_dim(k_pad, start, band, axis=1)
        vb = lax.dynamic_slice_in_dim(v_pad, start, band, axis=1)
        s = jnp.einsum('bqhd,bkhd->bhqk', qc, kb).astype(jnp.float32) * scale + bias
        valid = (start + key_off) >= 0
        s = jnp.where(valid[None, None, None, :], s, neg)
        pr = jax.nn.softmax(s, axis=-1).astype(vb.dtype)
        return jnp.einsum('bhqk,bkhd->bqhd', pr, vb)

    out = lax.map(one_chunk, (jnp.arange(nc), q_chunks))
    return out.transpose(1, 0, 2, 3, 4).reshape(B, T, H * hd)


def gated_linear_attention(q, k, v, log_a):
    B, T, H, dk = q.shape
    dv = v.shape[-1]
    nc = T // CHUNK

    def to_chunks(t):
        return t.reshape(B, nc, CHUNK, H, t.shape[-1]).transpose(0, 1, 3, 2, 4).astype(jnp.float32)

    qc = to_chunks(q) * (dk ** -0.5)
    kc = to_chunks(k)
    vc = to_chunks(v)
    b = jnp.cumsum(to_chunks(log_a), axis=3)
    b_last = b[:, :, :, -1:, :]
    q_dec = qc * jnp.exp(b)
    k_inv = kc * jnp.exp(-b)
    k_to_end = kc * jnp.exp(b_last - b)
    causal = jnp.tril(jnp.ones((CHUNK, CHUNK), dtype=bool))
    attn = jnp.where(causal, jnp.einsum('bnhid,bnhjd->bnhij', q_dec, k_inv), 0.0)
    o_intra = jnp.einsum('bnhij,bnhjv->bnhiv', attn, vc)

    chunk_update = jnp.einsum('bnhjd,bnhjv->bnhdv', k_to_end, vc)
    chunk_decay = jnp.exp(b_last[:, :, :, 0, :])

    def step(S, xs):
        upd, dec = xs
        return dec[..., None] * S + upd, S

    S0 = jnp.zeros((B, H, dk, dv), jnp.float32)
    _, S_prev = lax.scan(step, S0, (chunk_update.transpose(1, 0, 2, 3, 4),
                                    chunk_decay.transpose(1, 0, 2, 3)))
    S_prev = S_prev.transpose(1, 0, 2, 3, 4)
    o_inter = jnp.einsum('bnhid,bnhdv->bnhiv', q_dec, S_prev)
    o = o_intra + o_inter
    return o.transpose(0, 1, 3, 2, 4).reshape(B, T, H, dv)


def causal_depthwise_conv(u, w, bias):
    T = u.shape[1]
    up = jnp.pad(u, ((0, 0), (CONV_WIDTH - 1, 0), (0, 0)))
    y = bias
    for tap in range(CONV_WIDTH):
        y = y + up[:, tap:tap + T] * w[tap]
    return y


def setup_inputs(seed: int = 0) -> dict:
    key = jax.random.key(seed)
    ks = jax.random.split(key, 20)
    f32 = jnp.float32

    def nrm(k, shape, scale):
        return jax.random.normal(k, shape, f32) * scale

    def gain(k, shape):
        return 1.0 + 0.05 * jax.random.normal(k, shape, f32)

    return {
        "x": nrm(ks[0], (BATCH, SEQ, D_MODEL), 1.0),
        "p": nrm(ks[1], (DEPTH, BATCH, SEQ, PLE_DIM), 1.0),
        "norm_mix_w": gain(ks[2], (DEPTH, D_MODEL)),
        "w_in": nrm(ks[3], (DEPTH, D_MODEL, IN_PROJ_WIDTH), D_MODEL ** -0.5),
        "att_rel_bias": nrm(ks[4], (DEPTH, ATT_HEADS, 2 * MAX_REL_DIST + 1), 0.3),
        "w_gla_gate_up": nrm(ks[5], (DEPTH, GLA_GATE_RANK, GLA_KEY_WIDTH), GLA_GATE_RANK ** -0.5),
        "b_gla_gate": nrm(ks[6], (DEPTH, GLA_KEY_WIDTH), 0.5),
        "gla_norm_w": gain(ks[7], (DEPTH, GLA_DV)),
        "w_out": nrm(ks[8], (DEPTH, MIX_WIDTH, D_MODEL), MIX_WIDTH ** -0.5),
        "norm_ffn_w": gain(ks[9], (DEPTH, D_MODEL)),
        "w_ffn_up": nrm(ks[10], (DEPTH, D_MODEL, 2 * D_FF), D_MODEL ** -0.5),
        "w_ffn_conv": nrm(ks[11], (DEPTH, CONV_WIDTH, 2 * D_FF), CONV_WIDTH ** -0.5),
        "b_ffn_conv": nrm(ks[12], (DEPTH, 2 * D_FF), 0.02),
        "w_ffn_down": nrm(ks[13], (DEPTH, D_FF, D_MODEL), D_FF ** -0.5),
        "norm_ple_w": gain(ks[14], (DEPTH, D_MODEL)),
        "w_ple_gate": nrm(ks[15], (DEPTH, D_MODEL, D_MODEL), D_MODEL ** -0.5),
        "w_ple_proj": nrm(ks[16], (DEPTH, PLE_DIM, D_MODEL), PLE_DIM ** -0.5),
        "final_norm_w": gain(ks[17], (D_MODEL,)),
    }


def reference(x, p, norm_mix_w, w_in, att_rel_bias, w_gla_gate_up, b_gla_gate, gla_norm_w,
              w_out, norm_ffn_w, w_ffn_up, w_ffn_conv, b_ffn_conv, w_ffn_down,
              norm_ple_w, w_ple_gate, w_ple_proj, final_norm_w):
    B, T, _ = x.shape
    for i in range(DEPTH):
        h = rms_norm(x, norm_mix_w[i])
        proj = h @ w_in[i]
        q_a, k_a, v_a, q_g, k_g, v_g, r_g, gate_lr = split_cols(proj, IN_SPLITS)

        hs_a = (B, T, ATT_HEADS, ATT_HEAD_DIM)
        y_a = chunked_band_attention(q_a.reshape(hs_a), k_a.reshape(hs_a), v_a.reshape(hs_a),
                                     att_rel_bias[i])

        gate_logit = (gate_lr @ w_gla_gate_up[i] + b_gla_gate[i]).astype(jnp.float32)
        log_a = (jax.nn.log_sigmoid(gate_logit) / GLA_GATE_TAU).reshape(B, T, GLA_HEADS, GLA_DK)
        o_g = gated_linear_attention(q_g.reshape(B, T, GLA_HEADS, GLA_DK),
                                     k_g.reshape(B, T, GLA_HEADS, GLA_DK),
                                     v_g.reshape(B, T, GLA_HEADS, GLA_DV), log_a)
        o_g = rms_norm(o_g.astype(x.dtype), gla_norm_w[i]).reshape(B, T, GLA_WIDTH)
        y_g = o_g * jax.nn.silu(r_g)

        x = x + jnp.concatenate([y_a, y_g], axis=-1) @ w_out[i]

        h = rms_norm(x, norm_ffn_w[i])
        u = causal_depthwise_conv(h @ w_ffn_up[i], w_ffn_conv[i], b_ffn_conv[i])
        u_gate, u_val = jnp.split(u, 2, axis=-1)
        x = x + (jax.nn.gelu(u_gate) * u_val) @ w_ffn_down[i]

        h = rms_norm(x, norm_ple_w[i])
        g = jax.nn.sigmoid(h @ w_ple_gate[i])
        x = x + g * (p[i] @ w_ple_proj[i])
    return rms_norm(x, final_norm_w)
```

```python
import functools

import jax
import jax.numpy as jnp
import numpy as np
from jax import lax
from jax.experimental import pallas as pl
from jax.experimental.pallas import tpu as pltpu

F32 = jnp.float32
BF16 = jnp.bfloat16

D_MODEL = 2048
SEQ = 16384
CHUNK = 64
PLE_DIM = 256
EPS = 1e-6

ATT_HEAD_DIM = 64
ATT_WIDTH = 1024
ATT_HEADS = 16
BAND_CHUNKS = 9
MAX_REL_DIST = 256

GLA_HEADS = 4
GLA_WIDTH = 1024
GLA_DV = 256
GLA_KEY_WIDTH = 512
GLA_DK = 128
GLA_GATE_RANK = 16
GLA_GATE_TAU = 16.0

MAIN_PROJ_WIDTH = 3 * ATT_WIDTH + 2 * GLA_KEY_WIDTH + 2 * GLA_WIDTH
D_FF = 5632
CONV_WIDTH = 3

LANES = 128
BF16_SUBLANES = 16
VMEM_LIMIT = 56 * 1024 * 1024

TM_PROJ = 1024
TN_PROJ = 1024
ATT_ROWS = 256
ATT_KEYS = 3 * ATT_ROWS
GLA_ROWS = 256
TM_FFN = 512
TF_FFN = 512
HALO = BF16_SUBLANES
TM_ROW = 512


def _rms_norm(x, w):
    ms = jnp.mean(x * x, axis=-1, keepdims=True)
    return x * lax.rsqrt(ms + EPS) * w


def _dot(a, b):
    return jnp.dot(a, b, preferred_element_type=F32)


def _cparams(sem):
    return pltpu.CompilerParams(dimension_semantics=sem, vmem_limit_bytes=VMEM_LIMIT)


def _inproj_kernel(x_ref, nw_ref, w_ref, wlr_ref, wup_ref, bg_ref, proj_ref, loga_ref, h_ref):
    @pl.when(pl.program_id(1) == 0)
    def _():
        h = _rms_norm(x_ref[...], nw_ref[...]).astype(BF16)
        h_ref[...] = h
        gate_lr = _dot(h, wlr_ref[...]).astype(BF16)
        logit = _dot(gate_lr, wup_ref[...]) + bg_ref[...]
        log_sig = jnp.minimum(logit, 0.0) - jnp.log(1.0 + jnp.exp(-jnp.abs(logit)))
        loga_ref[...] = log_sig * (1.0 / GLA_GATE_TAU)

    proj_ref[...] = _dot(h_ref[...], w_ref[...]).astype(BF16)


def _inproj(x, nw, w_main, w_lr, w_up, b_gate):
    T = x.shape[0]
    grid = (T // TM_PROJ, MAIN_PROJ_WIDTH // TN_PROJ)
    return pl.pallas_call(
        _inproj_kernel,
        grid=grid,
        in_specs=[
            pl.BlockSpec((TM_PROJ, D_MODEL), lambda i, j: (i, 0)),
            pl.BlockSpec((1, D_MODEL), lambda i, j: (0, 0)),
            pl.BlockSpec((D_MODEL, TN_PROJ), lambda i, j: (0, j)),
            pl.BlockSpec((D_MODEL, LANES), lambda i, j: (0, 0)),
            pl.BlockSpec((LANES, GLA_KEY_WIDTH), lambda i, j: (0, 0)),
            pl.BlockSpec((1, GLA_KEY_WIDTH), lambda i, j: (0, 0)),
        ],
        out_specs=[
            pl.BlockSpec((TM_PROJ, TN_PROJ), lambda i, j: (i, j)),
            pl.BlockSpec((TM_PROJ, GLA_KEY_WIDTH), lambda i, j: (i, 0)),
        ],
        out_shape=[
            jax.ShapeDtypeStruct((T, MAIN_PROJ_WIDTH), BF16),
            jax.ShapeDtypeStruct((T, GLA_KEY_WIDTH), F32),
        ],
        scratch_shapes=[pltpu.VMEM((TM_PROJ, D_MODEL), BF16)],
        compiler_params=_cparams(("arbitrary", "arbitrary")),
        name="inproj",
    )(x, nw, w_main, w_lr, w_up, b_gate)


def _attn_kernel(q_ref, k0_ref, k1_ref, k2_ref, v0_ref, v1_ref, v2_ref, bias_ref, o_ref):
    g = pl.program_id(0)
    scale = ATT_HEAD_DIM ** -0.5
    col = lax.broadcasted_iota(jnp.int32, (ATT_ROWS, ATT_KEYS), 1)
    valid = col >= (2 - g) * ATT_ROWS
    lane = lax.broadcasted_iota(jnp.int32, (ATT_ROWS, LANES), 1)
    first_head = lane < ATT_HEAD_DIM
    nt = (((1,), (1,)), ((), ()))

    def pair_body(p, carry):
        cs = pl.ds(pl.multiple_of(p * LANES, LANES), LANES)
        q = q_ref[:, cs]
        ks = (k0_ref[:, cs], k1_ref[:, cs], k2_ref[:, cs])
        v = jnp.concatenate([v0_ref[:, cs], v1_ref[:, cs], v2_ref[:, cs]], axis=0)
        zero = jnp.zeros_like(q)
        outs = []
        for sub in range(2):
            qh = jnp.where(first_head, q, zero) if sub == 0 else jnp.where(first_head, zero, q)
            s = jnp.concatenate(
                [lax.dot_general(qh, kb, nt, preferred_element_type=F32) for kb in ks], axis=1)
            s = s * scale + bias_ref[2 * p + sub]
            s = jnp.where(valid, s, -jnp.inf)
            m = jnp.max(s, axis=-1, keepdims=True)
            e = jnp.exp(s - m)
            l = jnp.sum(e, axis=-1, keepdims=True)
            o = _dot(e.astype(BF16), v)
            outs.append(o / l)
        o_ref[:, cs] = jnp.where(first_head, outs[0], outs[1]).astype(o_ref.dtype)
        return carry

    lax.fori_loop(0, ATT_HEADS // 2, pair_body, 0)


def _attention(proj, bias_full):
    T = proj.shape[0]
    n = T // ATT_ROWS

    def kv_spec(back, colblk):
        return pl.BlockSpec((ATT_ROWS, ATT_WIDTH), lambda g: (jnp.maximum(g - back, 0), colblk))

    return pl.pallas_call(
        _attn_kernel,
        grid=(n,),
        in_specs=[
            pl.BlockSpec((ATT_ROWS, ATT_WIDTH), lambda g: (g, 0)),
            kv_spec(2, 1), kv_spec(1, 1), kv_spec(0, 1),
            kv_spec(2, 2), kv_spec(1, 2), kv_spec(0, 2),
            pl.BlockSpec((ATT_HEADS, ATT_ROWS, ATT_KEYS), lambda g: (0, 0, 0)),
        ],
        out_specs=pl.BlockSpec((ATT_ROWS, ATT_WIDTH), lambda g: (g, 0)),
        out_shape=jax.ShapeDtypeStruct((T, ATT_WIDTH), BF16),
        compiler_params=_cparams(("arbitrary",)),
        name="band_attention",
    )(proj, proj, proj, proj, proj, proj, proj, bias_full)


def _attention_bias(table):
    r = np.arange(ATT_ROWS)[:, None]
    c = np.arange(ATT_KEYS)[None, :]
    dist = np.clip(r - c + 2 * ATT_ROWS, -MAX_REL_DIST, MAX_REL_DIST) + MAX_REL_DIST
    qc = r // CHUNK
    kc = c // CHUNK
    in_band = (kc >= qc) & (kc <= qc + BAND_CHUNKS - 1)
    bias = table[:, dist]
    return jnp.where(in_band[None], bias, -jnp.inf).astype(F32)


def _gla_kernel(q_ref, k_ref, v_ref, r_ref, la_ref, nw_ref, o_ref, s_ref):
    @pl.when(pl.program_id(0) == 0)
    def _():
        s_ref[...] = jnp.zeros_like(s_ref)

    n_chunks = GLA_ROWS // CHUNK
    row = lax.broadcasted_iota(jnp.int32, (GLA_ROWS, GLA_ROWS), 0)
    col = lax.broadcasted_iota(jnp.int32, (GLA_ROWS, GLA_ROWS), 1)
    same_chunk = (row // CHUNK) == (col // CHUNK)
    causal = same_chunk & (col <= row)
    tri = jnp.where(causal, 1.0, 0.0).astype(BF16)
    blk = jnp.where(same_chunk, 1.0, 0.0).astype(BF16)

    la = la_ref[...]
    hi = la.astype(BF16)
    rem = la - hi.astype(F32)
    mid = rem.astype(BF16)
    lo = (rem - mid.astype(F32)).astype(BF16)
    b = _dot(tri, hi) + _dot(tri, mid) + _dot(tri, lo)
    b_last = _dot(blk, hi) + _dot(blk, mid) + _dot(blk, lo)

    q = q_ref[...].astype(F32) * (GLA_DK ** -0.5)
    k = k_ref[...].astype(F32)
    q_dec = (q * jnp.exp(b)).astype(BF16)
    k_inv = (k * jnp.exp(-b)).astype(BF16)
    k_end = (k * jnp.exp(b_last - b)).astype(BF16)
    nt = (((1,), (1,)), ((), ()))
    tn = (((0,), (0,)), ((), ()))

    for h in range(GLA_HEADS):
        ks = slice(h * GLA_DK, (h + 1) * GLA_DK)
        vs = slice(h * GLA_DV, (h + 1) * GLA_DV)
        v = v_ref[:, vs]
        attn = lax.dot_general(q_dec[:, ks], k_inv[:, ks], nt, preferred_element_type=F32)
        attn = jnp.where(causal, attn, 0.0).astype(BF16)
        o_intra = _dot(attn, v)

        state = s_ref[h]
        inter = []
        for c in range(n_chunks):
            rs = slice(c * CHUNK, (c + 1) * CHUNK)
            inter.append(_dot(q_dec[rs, ks], state.astype(BF16)))
            upd = lax.dot_general(k_end[rs, ks], v[rs], tn, preferred_element_type=F32)
            tot = jnp.broadcast_to(b_last[c * CHUNK:c * CHUNK + 1, ks], (GLA_DK, GLA_DK))
            decay = jnp.exp(tot.T)
            state = jnp.concatenate([decay, decay], axis=1) * state + upd
        s_ref[h] = state

        o = o_intra + jnp.concatenate(inter, axis=0)
        o = _rms_norm(o, nw_ref[...])
        r = r_ref[:, vs].astype(F32)
        o_ref[:, vs] = (o * (r * jax.nn.sigmoid(r))).astype(o_ref.dtype)


def _gla(proj, log_a, norm_w):
    T = proj.shape[0]
    return pl.pallas_call(
        _gla_kernel,
        grid=(T // GLA_ROWS,),
        in_specs=[
            pl.BlockSpec((GLA_ROWS, GLA_KEY_WIDTH), lambda g: (g, 3 * ATT_WIDTH // GLA_KEY_WIDTH)),
            pl.BlockSpec((GLA_ROWS, GLA_KEY_WIDTH), lambda g: (g, 3 * ATT_WIDTH // GLA_KEY_WIDTH + 1)),
            pl.BlockSpec((GLA_ROWS, GLA_WIDTH), lambda g: (g, 4)),
            pl.BlockSpec((GLA_ROWS, GLA_WIDTH), lambda g: (g, 5)),
            pl.BlockSpec((GLA_ROWS, GLA_KEY_WIDTH), lambda g: (g, 0)),
            pl.BlockSpec((1, GLA_DV), lambda g: (0, 0)),
        ],
        out_specs=pl.BlockSpec((GLA_ROWS, GLA_WIDTH), lambda g: (g, 0)),
        out_shape=jax.ShapeDtypeStruct((T, GLA_WIDTH), BF16),
        scratch_shapes=[pltpu.VMEM((GLA_HEADS, GLA_DK, GLA_DV), F32)],
        compiler_params=_cparams(("arbitrary",)),
        name="gla",
    )(proj, proj, proj, proj, log_a, norm_w)


def _outproj_kernel(x_ref, ya_ref, yg_ref, wa_ref, wg_ref, o_ref):
    o_ref[...] = x_ref[...] + _dot(ya_ref[...], wa_ref[...]) + _dot(yg_ref[...], wg_ref[...])


def _outproj(x, y_a, y_g, w_out):
    T = x.shape[0]
    return pl.pallas_call(
        _outproj_kernel,
        grid=(T // TM_ROW,),
        in_specs=[
            pl.BlockSpec((TM_ROW, D_MODEL), lambda i: (i, 0)),
            pl.BlockSpec((TM_ROW, ATT_WIDTH), lambda i: (i, 0)),
            pl.BlockSpec((TM_ROW, GLA_WIDTH), lambda i: (i, 0)),
            pl.BlockSpec((ATT_WIDTH, D_MODEL), lambda i: (0, 0)),
            pl.BlockSpec((GLA_WIDTH, D_MODEL), lambda i: (1, 0)),
        ],
        out_specs=pl.BlockSpec((TM_ROW, D_MODEL), lambda i: (i, 0)),
        out_shape=jax.ShapeDtypeStruct((T, D_MODEL), F32),
        compiler_params=_cparams(("arbitrary",)),
        name="outproj",
    )(x, y_a, y_g, w_out, w_out)


def _gelu_tanh(x):
    c = np.sqrt(2.0 / np.pi).astype(np.float32)
    return 0.5 * x * (1.0 + jnp.tanh(c * (x + 0.044715 * (x * x * x))))


def _causal_conv(u, w_ref, b_ref):
    n = u.shape[0] - HALO
    return (b_ref[...]
            + u[HALO - 2:HALO - 2 + n] * w_ref[0:1, :]
            + u[HALO - 1:HALO - 1 + n] * w_ref[1:2, :]
            + u[HALO:HALO + n] * w_ref[2:3, :])


def _ffn_kernel(x_ref, xh_ref, nw_ref, wg_ref, wv_ref, cwg_ref, cwv_ref, cbg_ref, cbv_ref, wd_ref,
                o_ref, h_ref):
    i = pl.program_id(0)

    @pl.when(pl.program_id(1) == 0)
    def _():
        x = x_ref[...]
        h_ref[HALO:, :] = _rms_norm(x, nw_ref[...]).astype(BF16)
        halo = _rms_norm(xh_ref[...], nw_ref[...])
        h_ref[:HALO, :] = jnp.where(i > 0, halo, 0.0).astype(BF16)
        o_ref[...] = x

    h = h_ref[...]
    u_gate = _causal_conv(_dot(h, wg_ref[...]), cwg_ref, cbg_ref)
    u_val = _causal_conv(_dot(h, wv_ref[...]), cwv_ref, cbv_ref)
    act = (_gelu_tanh(u_gate) * u_val).astype(BF16)
    o_ref[...] += _dot(act, wd_ref[...])


def _ffn(x, nw, w_up, w_conv, b_conv, w_down):
    T = x.shape[0]
    nf = D_FF // TF_FFN
    halo_blocks = TM_FFN // HALO
    return pl.pallas_call(
        _ffn_kernel,
        grid=(T // TM_FFN, nf),
        in_specs=[
            pl.BlockSpec((TM_FFN, D_MODEL), lambda i, j: (i, 0)),
            pl.BlockSpec((HALO, D_MODEL), lambda i, j: (jnp.maximum(i * halo_blocks - 1, 0), 0)),
            pl.BlockSpec((1, D_MODEL), lambda i, j: (0, 0)),
            pl.BlockSpec((D_MODEL, TF_FFN), lambda i, j: (0, j)),
            pl.BlockSpec((D_MODEL, TF_FFN), lambda i, j: (0, nf + j)),
            pl.BlockSpec((CONV_WIDTH, TF_FFN), lambda i, j: (0, j)),
            pl.BlockSpec((CONV_WIDTH, TF_FFN), lambda i, j: (0, nf + j)),
            pl.BlockSpec((1, TF_FFN), lambda i, j: (0, j)),
            pl.BlockSpec((1, TF_FFN), lambda i, j: (0, nf + j)),
            pl.BlockSpec((TF_FFN, D_MODEL), lambda i, j: (j, 0)),
        ],
        out_specs=pl.BlockSpec((TM_FFN, D_MODEL), lambda i, j: (i, 0)),
        out_shape=jax.ShapeDtypeStruct((T, D_MODEL), F32),
        scratch_shapes=[pltpu.VMEM((TM_FFN + HALO, D_MODEL), BF16)],
        compiler_params=_cparams(("arbitrary", "arbitrary")),
        name="conv_ffn",
    )(x, x, nw, w_up, w_up, w_conv, w_conv, b_conv, b_conv, w_down)


def _ple_kernel(x_ref, p_ref, nw_ref, wg_ref, wp_ref, fw_ref, o_ref):
    x = x_ref[...]
    h = _rms_norm(x, nw_ref[...]).astype(BF16)
    gate = jax.nn.sigmoid(_dot(h, wg_ref[...]))
    emb = _dot(p_ref[...].astype(BF16), wp_ref[...])
    o_ref[...] = _rms_norm(x + gate * emb, fw_ref[...])


def _ple(x, p, nw, w_gate, w_proj, final_w):
    T = x.shape[0]
    return pl.pallas_call(
        _ple_kernel,
        grid=(T // TM_ROW,),
        in_specs=[
            pl.BlockSpec((TM_ROW, D_MODEL), lambda i: (i, 0)),
            pl.BlockSpec((TM_ROW, PLE_DIM), lambda i: (i, 0)),
            pl.BlockSpec((1, D_MODEL), lambda i: (0, 0)),
            pl.BlockSpec((D_MODEL, D_MODEL), lambda i: (0, 0)),
            pl.BlockSpec((PLE_DIM, D_MODEL), lambda i: (0, 0)),
            pl.BlockSpec((1, D_MODEL), lambda i: (0, 0)),
        ],
        out_specs=pl.BlockSpec((TM_ROW, D_MODEL), lambda i: (i, 0)),
        out_shape=jax.ShapeDtypeStruct((T, D_MODEL), F32),
        compiler_params=_cparams(("arbitrary",)),
        name="ple_final",
    )(x, p, nw, w_gate, w_proj, final_w)


def kernel(x, p, norm_mix_w, w_in, att_rel_bias, w_gla_gate_up, b_gla_gate, gla_norm_w, w_out, norm_ffn_w, w_ffn_up, w_ffn_conv, b_ffn_conv, w_ffn_down, norm_ple_w, w_ple_gate, w_ple_proj, final_norm_w):
    B, T, D = x.shape
    assert (B, T, D) == (1, SEQ, D_MODEL) and w_in.shape[0] == 1
    xs = x[0]
    i = 0
    w_main = w_in[i, :, :MAIN_PROJ_WIDTH].astype(BF16)
    w_lr = jnp.pad(w_in[i, :, MAIN_PROJ_WIDTH:], ((0, 0), (0, LANES - GLA_GATE_RANK))).astype(BF16)
    w_up = jnp.pad(w_gla_gate_up[i], ((0, LANES - GLA_GATE_RANK), (0, 0))).astype(BF16)

    proj, log_a = _inproj(xs, norm_mix_w[i][None], w_main, w_lr, w_up, b_gla_gate[i][None])
    y_a = _attention(proj, _attention_bias(att_rel_bias[i]))
    y_g = _gla(proj, log_a, gla_norm_w[i][None])
    xs = _outproj(xs, y_a, y_g, w_out[i].astype(BF16))
    xs = _ffn(xs, norm_ffn_w[i][None], w_ffn_up[i].astype(BF16), w_ffn_conv[i], b_ffn_conv[i][None],
              w_ffn_down[i].astype(BF16))
    xs = _ple(xs, p[i, 0], norm_ple_w[i][None], w_ple_gate[i].astype(BF16), w_ple_proj[i].astype(BF16),
              final_norm_w[None])
    return xs[None]
```

```python
import functools

import jax
import jax.numpy as jnp
import numpy as np
from jax import lax
from jax.experimental import pallas as pl
from jax.experimental.pallas import tpu as pltpu

F32 = jnp.float32
BF16 = jnp.bfloat16

D_MODEL = 2048
SEQ = 16384
CHUNK = 64
PLE_DIM = 256
EPS = 1e-6

ATT_HEAD_DIM = 64
ATT_WIDTH = 1024
ATT_HEADS = 16
BAND_CHUNKS = 9
MAX_REL_DIST = 256

GLA_HEADS = 4
GLA_WIDTH = 1024
GLA_DV = 256
GLA_KEY_WIDTH = 512
GLA_DK = 128
GLA_GATE_RANK = 16
GLA_GATE_TAU = 16.0

MAIN_PROJ_WIDTH = 3 * ATT_WIDTH + 2 * GLA_KEY_WIDTH + 2 * GLA_WIDTH
D_FF = 5632
CONV_WIDTH = 3

LANES = 128
BF16_SUBLANES = 16
VMEM_LIMIT = 56 * 1024 * 1024

TM_PROJ = 1024
TN_PROJ = 1024
ATT_ROWS = 256
ATT_WIN = ATT_ROWS // 2 + (BAND_CHUNKS - 1) * CHUNK
GLA_ROWS = 256
TM_FFN = 512
TF_FFN = 512
HALO = BF16_SUBLANES
TM_ROW = 512


def _rms_norm(x, w):
    ms = jnp.mean(x * x, axis=-1, keepdims=True)
    return x * lax.rsqrt(ms + EPS) * w


def _dot(a, b):
    return jnp.dot(a, b, preferred_element_type=F32)


def _cparams(sem):
    return pltpu.CompilerParams(dimension_semantics=sem, vmem_limit_bytes=VMEM_LIMIT)


def _inproj_kernel(x_ref, nw_ref, w_ref, wlr_ref, wup_ref, bg_ref, proj_ref, loga_ref, h_ref):
    @pl.when(pl.program_id(1) == 0)
    def _():
        h = _rms_norm(x_ref[...], nw_ref[...]).astype(BF16)
        h_ref[...] = h
        gate_lr = _dot(h, wlr_ref[...]).astype(BF16)
        logit = _dot(gate_lr, wup_ref[...]) + bg_ref[...]
        log_sig = jnp.minimum(logit, 0.0) - jnp.log(1.0 + jnp.exp(-jnp.abs(logit)))
        loga_ref[...] = log_sig * (1.0 / GLA_GATE_TAU)

    proj_ref[...] = _dot(h_ref[...], w_ref[...]).astype(BF16)


def _inproj(x, nw, w_main, w_lr, w_up, b_gate):
    T = x.shape[0]
    grid = (T // TM_PROJ, MAIN_PROJ_WIDTH // TN_PROJ)
    return pl.pallas_call(
        _inproj_kernel,
        grid=grid,
        in_specs=[
            pl.BlockSpec((TM_PROJ, D_MODEL), lambda i, j: (i, 0)),
            pl.BlockSpec((1, D_MODEL), lambda i, j: (0, 0)),
            pl.BlockSpec((D_MODEL, TN_PROJ), lambda i, j: (0, j)),
            pl.BlockSpec((D_MODEL, LANES), lambda i, j: (0, 0)),
            pl.BlockSpec((LANES, GLA_KEY_WIDTH), lambda i, j: (0, 0)),
            pl.BlockSpec((1, GLA_KEY_WIDTH), lambda i, j: (0, 0)),
        ],
        out_specs=[
            pl.BlockSpec((TM_PROJ, TN_PROJ), lambda i, j: (i, j)),
            pl.BlockSpec((TM_PROJ, GLA_KEY_WIDTH), lambda i, j: (i, 0)),
        ],
        out_shape=[
            jax.ShapeDtypeStruct((T, MAIN_PROJ_WIDTH), BF16),
            jax.ShapeDtypeStruct((T, GLA_KEY_WIDTH), F32),
        ],
        scratch_shapes=[pltpu.VMEM((TM_PROJ, D_MODEL), BF16)],
        compiler_params=_cparams(("arbitrary", "arbitrary")),
        name="inproj",
    )(x, nw, w_main, w_lr, w_up, b_gate)


def _attn_pairs(q_ref, k_refs, v_refs, bias_ref, o_ref, first_frame):
    half = ATT_ROWS // 2
    lane = lax.broadcasted_iota(jnp.int32, (ATT_ROWS, LANES), 1)
    first_head = lane < ATT_HEAD_DIM
    first_head_v = lax.broadcasted_iota(jnp.int32, (3 * ATT_ROWS, LANES), 1) < ATT_HEAD_DIM
    nt = (((1,), (1,)), ((), ()))
    if first_frame is not None:
        col = lax.broadcasted_iota(jnp.int32, (half, ATT_WIN), 1)
        valid = (col >= -first_frame, col >= -first_frame - half)
    scale = jnp.asarray(ATT_HEAD_DIM ** -0.5, BF16)
    zero_blk = jnp.zeros((half, half), BF16)

    def scores(hd):
        cs = slice(hd // 2 * LANES, (hd // 2 + 1) * LANES)
        q = q_ref[:, cs] * scale
        zero = jnp.zeros_like(q)
        qh = jnp.where(first_head, q, zero) if hd % 2 == 0 else jnp.where(first_head, zero, q)
        s0, s1, s2 = [lax.dot_general(qh, r[:, cs], nt, preferred_element_type=F32) for r in k_refs]
        return (jnp.concatenate([s0[:half], s1[:half], s2[:half, :half]], axis=1),
                jnp.concatenate([s0[half:, half:], s1[half:], s2[half:]], axis=1))

    def weights(wins, hd):
        es = []
        for w in range(2):
            s = wins[w] + bias_ref[hd]
            if first_frame is not None:
                s = jnp.where(valid[w], s, -jnp.inf)
            m = jnp.max(s, axis=-1, keepdims=True)
            es.append(jnp.exp(s - m).astype(BF16))
        return jnp.concatenate([jnp.concatenate([es[0], zero_blk], axis=1),
                                jnp.concatenate([zero_blk, es[1]], axis=1)], axis=0)

    def values(e, hd):
        cs = slice(hd // 2 * LANES, (hd // 2 + 1) * LANES)
        v = jnp.concatenate([r[:, cs] for r in v_refs], axis=0)
        one = jnp.ones_like(v)
        vh = jnp.where(first_head_v, v, one) if hd % 2 == 0 else jnp.where(first_head_v, one, v)
        o = _dot(e, vh)
        return o / pltpu.roll(o, ATT_HEAD_DIM, 1)

    wins = scores(0)
    prev = None
    for hd in range(ATT_HEADS):
        nxt = scores(hd + 1) if hd + 1 < ATT_HEADS else None
        o = values(weights(wins, hd), hd)
        if hd % 2 == 1:
            cs = slice(hd // 2 * LANES, (hd // 2 + 1) * LANES)
            o_ref[:, cs] = jnp.where(first_head, prev, o).astype(o_ref.dtype)
        prev, wins = o, nxt


def _attn_kernel(q_ref, k0_ref, k1_ref, k2_ref, v0_ref, v1_ref, v2_ref, bias_ref, o_ref):
    g = pl.program_id(0)
    args = (q_ref, (k0_ref, k1_ref, k2_ref), (v0_ref, v1_ref, v2_ref), bias_ref, o_ref)

    @pl.when(g < 2)
    def _():
        _attn_pairs(*args, first_frame=(g - 2) * ATT_ROWS)

    @pl.when(g >= 2)
    def _():
        _attn_pairs(*args, first_frame=None)


def _attention(proj, bias_full):
    T = proj.shape[0]
    n = T // ATT_ROWS

    def kv_spec(back, colblk):
        return pl.BlockSpec((ATT_ROWS, ATT_WIDTH), lambda g: (jnp.maximum(g - back, 0), colblk))

    return pl.pallas_call(
        _attn_kernel,
        grid=(n,),
        in_specs=[
            pl.BlockSpec((ATT_ROWS, ATT_WIDTH), lambda g: (g, 0)),
            kv_spec(2, 1), kv_spec(1, 1), kv_spec(0, 1),
            kv_spec(2, 2), kv_spec(1, 2), kv_spec(0, 2),
            pl.BlockSpec((ATT_HEADS, ATT_ROWS // 2, ATT_WIN), lambda g: (0, 0, 0)),
        ],
        out_specs=pl.BlockSpec((ATT_ROWS, ATT_WIDTH), lambda g: (g, 0)),
        out_shape=jax.ShapeDtypeStruct((T, ATT_WIDTH), BF16),
        compiler_params=_cparams(("arbitrary",)),
        name="band_attention",
    )(proj, proj, proj, proj, proj, proj, proj, bias_full)


def _attention_bias(table):
    rows, off, md = ATT_ROWS // 2, 2 * ATT_ROWS, MAX_REL_DIST
    n_diag = rows + ATT_WIN - 1
    n_clipped = off - md + rows
    lowest = off + md - (ATT_WIN - 1)
    assert 0 <= lowest and n_diag - n_clipped == 2 * md - lowest
    H = table.shape[0]
    v = jnp.concatenate([
        jnp.broadcast_to(table[:, 2 * md:], (H, n_clipped)),
        table[:, lowest:2 * md][:, ::-1],
    ], axis=1)
    flat = jnp.tile(v, (1, rows))
    bias = flat[:, rows - 1:rows - 1 + rows * (n_diag - 1)]
    bias = bias.reshape(H, rows, n_diag - 1)[:, :, :ATT_WIN]
    qc = np.arange(rows)[:, None] // CHUNK
    kc = np.arange(ATT_WIN)[None, :] // CHUNK
    in_band = (kc >= qc) & (kc <= qc + BAND_CHUNKS - 1)
    return jnp.where(in_band[None], bias, -jnp.inf).astype(F32)


def _gla_kernel(q_ref, k_ref, v_ref, r_ref, la_ref, nw_ref, o_ref, s_ref):
    @pl.when(pl.program_id(0) == 0)
    def _():
        s_ref[...] = jnp.zeros_like(s_ref)

    n_chunks = GLA_ROWS // CHUNK
    row = lax.broadcasted_iota(jnp.int32, (GLA_ROWS, GLA_ROWS), 0)
    col = lax.broadcasted_iota(jnp.int32, (GLA_ROWS, GLA_ROWS), 1)
    same_chunk = (row // CHUNK) == (col // CHUNK)
    causal = same_chunk & (col <= row)
    tri = jnp.where(causal, 1.0, 0.0).astype(BF16)
    blk = jnp.where(same_chunk, 1.0, 0.0).astype(BF16)

    la = la_ref[...]
    hi = la.astype(BF16)
    rem = la - hi.astype(F32)
    mid = rem.astype(BF16)
    lo = (rem - mid.astype(F32)).astype(BF16)
    b = _dot(tri, hi) + _dot(tri, mid) + _dot(tri, lo)
    b_last = _dot(blk, hi) + _dot(blk, mid) + _dot(blk, lo)

    q = q_ref[...].astype(F32) * (GLA_DK ** -0.5)
    k = k_ref[...].astype(F32)
    q_dec = (q * jnp.exp(b)).astype(BF16)
    k_inv = (k * jnp.exp(-b)).astype(BF16)
    k_end = (k * jnp.exp(b_last - b)).astype(BF16)
    nt = (((1,), (1,)), ((), ()))
    tn = (((0,), (0,)), ((), ()))

    for h in range(GLA_HEADS):
        ks = slice(h * GLA_DK, (h + 1) * GLA_DK)
        vs = slice(h * GLA_DV, (h + 1) * GLA_DV)
        v = v_ref[:, vs]
        attn = lax.dot_general(q_dec[:, ks], k_inv[:, ks], nt, preferred_element_type=F32)
        attn = jnp.where(causal, attn, 0.0).astype(BF16)
        o_intra = _dot(attn, v)

        state = s_ref[h]
        inter = []
        for c in range(n_chunks):
            rs = slice(c * CHUNK, (c + 1) * CHUNK)
            inter.append(_dot(q_dec[rs, ks], state.astype(BF16)))
            upd = lax.dot_general(k_end[rs, ks], v[rs], tn, preferred_element_type=F32)
            tot = jnp.broadcast_to(b_last[c * CHUNK:c * CHUNK + 1, ks], (GLA_DK, GLA_DK))
            decay = jnp.exp(tot.T)
            state = jnp.concatenate([decay, decay], axis=1) * state + upd
        s_ref[h] = state

        o = o_intra + jnp.concatenate(inter, axis=0)
        o = _rms_norm(o, nw_ref[...])
        r = r_ref[:, vs].astype(F32)
        o_ref[:, vs] = (o * (r * jax.nn.sigmoid(r))).astype(o_ref.dtype)


def _gla(proj, log_a, norm_w):
    T = proj.shape[0]
    return pl.pallas_call(
        _gla_kernel,
        grid=(T // GLA_ROWS,),
        in_specs=[
            pl.BlockSpec((GLA_ROWS, GLA_KEY_WIDTH), lambda g: (g, 3 * ATT_WIDTH // GLA_KEY_WIDTH)),
            pl.BlockSpec((GLA_ROWS, GLA_KEY_WIDTH), lambda g: (g, 3 * ATT_WIDTH // GLA_KEY_WIDTH + 1)),
            pl.BlockSpec((GLA_ROWS, GLA_WIDTH), lambda g: (g, 4)),
            pl.BlockSpec((GLA_ROWS, GLA_WIDTH), lambda g: (g, 5)),
            pl.BlockSpec((GLA_ROWS, GLA_KEY_WIDTH), lambda g: (g, 0)),
            pl.BlockSpec((1, GLA_DV), lambda g: (0, 0)),
        ],
        out_specs=pl.BlockSpec((GLA_ROWS, GLA_WIDTH), lambda g: (g, 0)),
        out_shape=jax.ShapeDtypeStruct((T, GLA_WIDTH), BF16),
        scratch_shapes=[pltpu.VMEM((GLA_HEADS, GLA_DK, GLA_DV), F32)],
        compiler_params=_cparams(("arbitrary",)),
        name="gla",
    )(proj, proj, proj, proj, log_a, norm_w)


def _outproj_kernel(x_ref, ya_ref, yg_ref, wa_ref, wg_ref, o_ref):
    o_ref[...] = x_ref[...] + _dot(ya_ref[...], wa_ref[...]) + _dot(yg_ref[...], wg_ref[...])


def _outproj(x, y_a, y_g, w_out):
    T = x.shape[0]
    return pl.pallas_call(
        _outproj_kernel,
        grid=(T // TM_ROW,),
        in_specs=[
            pl.BlockSpec((TM_ROW, D_MODEL), lambda i: (i, 0)),
            pl.BlockSpec((TM_ROW, ATT_WIDTH), lambda i: (i, 0)),
            pl.BlockSpec((TM_ROW, GLA_WIDTH), lambda i: (i, 0)),
            pl.BlockSpec((ATT_WIDTH, D_MODEL), lambda i: (0, 0)),
            pl.BlockSpec((GLA_WIDTH, D_MODEL), lambda i: (1, 0)),
        ],
        out_specs=pl.BlockSpec((TM_ROW, D_MODEL), lambda i: (i, 0)),
        out_shape=jax.ShapeDtypeStruct((T, D_MODEL), F32),
        compiler_params=_cparams(("arbitrary",)),
        name="outproj",
    )(x, y_a, y_g, w_out, w_out)


def _gelu_tanh(x):
    c = np.sqrt(2.0 / np.pi).astype(np.float32)
    return 0.5 * x * (1.0 + jnp.tanh(c * (x + 0.044715 * (x * x * x))))


def _causal_conv(u, w_ref, b_ref):
    n = u.shape[0] - HALO
    return (b_ref[...]
            + u[HALO - 2:HALO - 2 + n] * w_ref[0:1, :]
            + u[HALO - 1:HALO - 1 + n] * w_ref[1:2, :]
            + u[HALO:HALO + n] * w_ref[2:3, :])


def _ffn_kernel(x_ref, xh_ref, nw_ref, wg_ref, wv_ref, cwg_ref, cwv_ref, cbg_ref, cbv_ref, wd_ref,
                o_ref, h_ref):
    i = pl.program_id(0)

    @pl.when(pl.program_id(1) == 0)
    def _():
        x = x_ref[...]
        h_ref[HALO:, :] = _rms_norm(x, nw_ref[...]).astype(BF16)
        halo = _rms_norm(xh_ref[...], nw_ref[...])
        h_ref[:HALO, :] = jnp.where(i > 0, halo, 0.0).astype(BF16)
        o_ref[...] = x

    h = h_ref[...]
    u_gate = _causal_conv(_dot(h, wg_ref[...]), cwg_ref, cbg_ref)
    u_val = _causal_conv(_dot(h, wv_ref[...]), cwv_ref, cbv_ref)
    act = (_gelu_tanh(u_gate) * u_val).astype(BF16)
    o_ref[...] += _dot(act, wd_ref[...])


def _ffn(x, nw, w_up, w_conv, b_conv, w_down):
    T = x.shape[0]
    nf = D_FF // TF_FFN
    halo_blocks = TM_FFN // HALO
    return pl.pallas_call(
        _ffn_kernel,
        grid=(T // TM_FFN, nf),
        in_specs=[
            pl.BlockSpec((TM_FFN, D_MODEL), lambda i, j: (i, 0)),
            pl.BlockSpec((HALO, D_MODEL), lambda i, j: (jnp.maximum(i * halo_blocks - 1, 0), 0)),
            pl.BlockSpec((1, D_MODEL), lambda i, j: (0, 0)),
            pl.BlockSpec((D_MODEL, TF_FFN), lambda i, j: (0, j)),
            pl.BlockSpec((D_MODEL, TF_FFN), lambda i, j: (0, nf + j)),
            pl.BlockSpec((CONV_WIDTH, TF_FFN), lambda i, j: (0, j)),
            pl.BlockSpec((CONV_WIDTH, TF_FFN), lambda i, j: (0, nf + j)),
            pl.BlockSpec((1, TF_FFN), lambda i, j: (0, j)),
            pl.BlockSpec((1, TF_FFN), lambda i, j: (0, nf + j)),
            pl.BlockSpec((TF_FFN, D_MODEL), lambda i, j: (j, 0)),
        ],
        out_specs=pl.BlockSpec((TM_FFN, D_MODEL), lambda i, j: (i, 0)),
        out_shape=jax.ShapeDtypeStruct((T, D_MODEL), F32),
        scratch_shapes=[pltpu.VMEM((TM_FFN + HALO, D_MODEL), BF16)],
        compiler_params=_cparams(("arbitrary", "arbitrary")),
        name="conv_ffn",
    )(x, x, nw, w_up, w_up, w_conv, w_conv, b_conv, b_conv, w_down)


def _ple_kernel(x_ref, p_ref, nw_ref, wg_ref, wp_ref, fw_ref, o_ref):
    x = x_ref[...]
    h = _rms_norm(x, nw_ref[...]).astype(BF16)
    gate = jax.nn.sigmoid(_dot(h, wg_ref[...]))
    emb = _dot(p_ref[...].astype(BF16), wp_ref[...])
    o_ref[...] = _rms_norm(x + gate * emb, fw_ref[...])


def _ple(x, p, nw, w_gate, w_proj, final_w):
    T = x.shape[0]
    return pl.pallas_call(
        _ple_kernel,
        grid=(T // TM_ROW,),
        in_specs=[
            pl.BlockSpec((TM_ROW, D_MODEL), lambda i: (i, 0)),
            pl.BlockSpec((TM_ROW, PLE_DIM), lambda i: (i, 0)),
            pl.BlockSpec((1, D_MODEL), lambda i: (0, 0)),
            pl.BlockSpec((D_MODEL, D_MODEL), lambda i: (0, 0)),
            pl.BlockSpec((PLE_DIM, D_MODEL), lambda i: (0, 0)),
            pl.BlockSpec((1, D_MODEL), lambda i: (0, 0)),
        ],
        out_specs=pl.BlockSpec((TM_ROW, D_MODEL), lambda i: (i, 0)),
        out_shape=jax.ShapeDtypeStruct((T, D_MODEL), F32),
        compiler_params=_cparams(("arbitrary",)),
        name="ple_final",
    )(x, p, nw, w_gate, w_proj, final_w)


def kernel(x, p, norm_mix_w, w_in, att_rel_bias, w_gla_gate_up, b_gla_gate, gla_norm_w, w_out, norm_ffn_w, w_ffn_up, w_ffn_conv, b_ffn_conv, w_ffn_down, norm_ple_w, w_ple_gate, w_ple_proj, final_norm_w):
    B, T, D = x.shape
    assert (B, T, D) == (1, SEQ, D_MODEL) and w_in.shape[0] == 1
    xs = x[0]
    i = 0
    w_main = w_in[i, :, :MAIN_PROJ_WIDTH].astype(BF16)
    w_lr = jnp.pad(w_in[i, :, MAIN_PROJ_WIDTH:], ((0, 0), (0, LANES - GLA_GATE_RANK))).astype(BF16)
    w_up = jnp.pad(w_gla_gate_up[i], ((0, LANES - GLA_GATE_RANK), (0, 0))).astype(BF16)

    proj, log_a = _inproj(xs, norm_mix_w[i][None], w_main, w_lr, w_up, b_gla_gate[i][None])
    y_a = _attention(proj, _attention_bias(att_rel_bias[i]))
    y_g = _gla(proj, log_a, gla_norm_w[i][None])
    xs = _outproj(xs, y_a, y_g, w_out[i].astype(BF16))
    xs = _ffn(xs, norm_ffn_w[i][None], w_ffn_up[i].astype(BF16), w_ffn_conv[i], b_ffn_conv[i][None],
              w_ffn_down[i].astype(BF16))
    xs = _ple(xs, p[i, 0], norm_ple_w[i][None], w_ple_gate[i].astype(BF16), w_ple_proj[i].astype(BF16),
              final_norm_w[None])
    return xs[None]
```

```python
import functools

import jax
import jax.numpy as jnp
import numpy as np
from jax import lax
from jax.experimental import pallas as pl
from jax.experimental.pallas import tpu as pltpu

F32 = jnp.float32
BF16 = jnp.bfloat16

D_MODEL = 2048
SEQ = 16384
CHUNK = 64
PLE_DIM = 256
EPS = 1e-6

ATT_HEAD_DIM = 64
ATT_WIDTH = 1024
ATT_HEADS = 16
BAND_CHUNKS = 9
MAX_REL_DIST = 256

GLA_HEADS = 4
GLA_WIDTH = 1024
GLA_DV = 256
GLA_KEY_WIDTH = 512
GLA_DK = 128
GLA_GATE_RANK = 16
GLA_GATE_TAU = 16.0

MAIN_PROJ_WIDTH = 3 * ATT_WIDTH + 2 * GLA_KEY_WIDTH + 2 * GLA_WIDTH
D_FF = 5632
CONV_WIDTH = 3

LANES = 128
BF16_SUBLANES = 16
VMEM_LIMIT = 56 * 1024 * 1024

TM_PROJ = 1024
TN_PROJ = 1024
ATT_ROWS = 256
ATT_WIN = ATT_ROWS // 2 + (BAND_CHUNKS - 1) * CHUNK
GLA_ROWS = 256
TM_FFN = 512
TF_FFN = 512
SUB_FFN = TF_FFN // 2
FFN_PIECES = 4
FFN_AHEAD = 2
HALO = BF16_SUBLANES
TM_ROW = 512


def _rms_norm(x, w):
    ms = jnp.mean(x * x, axis=-1, keepdims=True)
    return x * lax.rsqrt(ms + EPS) * w


def _dot(a, b):
    return jnp.dot(a, b, preferred_element_type=F32)


def _pack_rows(w):
    k, n = w.shape
    pairs = jnp.swapaxes(w.astype(BF16).reshape(k // 2, 2, n), 1, 2)
    return lax.bitcast_convert_type(pairs, jnp.uint32)


def _unpack_rows(w):
    return pltpu.bitcast(w, BF16)


def _cparams(sem):
    return pltpu.CompilerParams(dimension_semantics=sem, vmem_limit_bytes=VMEM_LIMIT)


def _inproj_kernel(x_ref, nw_ref, w_ref, wlr_ref, wup_ref, bg_ref, proj_ref, loga_ref, h_ref):
    @pl.when(pl.program_id(1) == 0)
    def _():
        h = _rms_norm(x_ref[...], nw_ref[...]).astype(BF16)
        h_ref[...] = h
        gate_lr = _dot(h, wlr_ref[...]).astype(BF16)
        logit = _dot(gate_lr, wup_ref[...]) + bg_ref[...]
        log_sig = jnp.minimum(logit, 0.0) - jnp.log(1.0 + jnp.exp(-jnp.abs(logit)))
        loga_ref[...] = log_sig * (1.0 / GLA_GATE_TAU)

    proj_ref[...] = _dot(h_ref[...], w_ref[...]).astype(BF16)


def _inproj(x, nw, w_main, w_lr, w_up, b_gate):
    T = x.shape[0]
    grid = (T // TM_PROJ, MAIN_PROJ_WIDTH // TN_PROJ)
    return pl.pallas_call(
        _inproj_kernel,
        grid=grid,
        in_specs=[
            pl.BlockSpec((TM_PROJ, D_MODEL), lambda i, j: (i, 0)),
            pl.BlockSpec((1, D_MODEL), lambda i, j: (0, 0)),
            pl.BlockSpec((D_MODEL, TN_PROJ), lambda i, j: (0, j)),
            pl.BlockSpec((D_MODEL, LANES), lambda i, j: (0, 0)),
            pl.BlockSpec((LANES, GLA_KEY_WIDTH), lambda i, j: (0, 0)),
            pl.BlockSpec((1, GLA_KEY_WIDTH), lambda i, j: (0, 0)),
        ],
        out_specs=[
            pl.BlockSpec((TM_PROJ, TN_PROJ), lambda i, j: (i, j)),
            pl.BlockSpec((TM_PROJ, GLA_KEY_WIDTH), lambda i, j: (i, 0)),
        ],
        out_shape=[
            jax.ShapeDtypeStruct((T, MAIN_PROJ_WIDTH), BF16),
            jax.ShapeDtypeStruct((T, GLA_KEY_WIDTH), F32),
        ],
        scratch_shapes=[pltpu.VMEM((TM_PROJ, D_MODEL), BF16)],
        compiler_params=_cparams(("arbitrary", "arbitrary")),
        name="inproj",
    )(x, nw, w_main, w_lr, w_up, b_gate)


def _attn_pairs(q_ref, k_refs, v_refs, bias_ref, o_ref, first_frame):
    half = ATT_ROWS // 2
    lane = lax.broadcasted_iota(jnp.int32, (ATT_ROWS, LANES), 1)
    first_head = lane < ATT_HEAD_DIM
    first_head_v = lax.broadcasted_iota(jnp.int32, (3 * ATT_ROWS, LANES), 1) < ATT_HEAD_DIM
    nt = (((1,), (1,)), ((), ()))
    if first_frame is not None:
        col = lax.broadcasted_iota(jnp.int32, (half, ATT_WIN), 1)
        valid = (col >= -first_frame, col >= -first_frame - half)
    scale = jnp.asarray(ATT_HEAD_DIM ** -0.5, BF16)
    zero_blk = jnp.zeros((half, half), BF16)

    def scores(hd):
        cs = slice(hd // 2 * LANES, (hd // 2 + 1) * LANES)
        q = q_ref[:, cs] * scale
        zero = jnp.zeros_like(q)
        qh = jnp.where(first_head, q, zero) if hd % 2 == 0 else jnp.where(first_head, zero, q)
        s0, s1, s2 = [lax.dot_general(qh, r[:, cs], nt, preferred_element_type=F32) for r in k_refs]
        return (jnp.concatenate([s0[:half], s1[:half], s2[:half, :half]], axis=1),
                jnp.concatenate([s0[half:, half:], s1[half:], s2[half:]], axis=1))

    def weights(wins, hd):
        es = []
        for w in range(2):
            s = wins[w] + bias_ref[hd]
            if first_frame is not None:
                s = jnp.where(valid[w], s, -jnp.inf)
            m = jnp.max(s, axis=-1, keepdims=True)
            es.append(jnp.exp(s - m).astype(BF16))
        return jnp.concatenate([jnp.concatenate([es[0], zero_blk], axis=1),
                                jnp.concatenate([zero_blk, es[1]], axis=1)], axis=0)

    def values(e, hd):
        cs = slice(hd // 2 * LANES, (hd // 2 + 1) * LANES)
        v = jnp.concatenate([r[:, cs] for r in v_refs], axis=0)
        one = jnp.ones_like(v)
        vh = jnp.where(first_head_v, v, one) if hd % 2 == 0 else jnp.where(first_head_v, one, v)
        o = _dot(e, vh)
        return o / pltpu.roll(o, ATT_HEAD_DIM, 1)

    wins = scores(0)
    prev = None
    for hd in range(ATT_HEADS):
        nxt = scores(hd + 1) if hd + 1 < ATT_HEADS else None
        o = values(weights(wins, hd), hd)
        if hd % 2 == 1:
            cs = slice(hd // 2 * LANES, (hd // 2 + 1) * LANES)
            o_ref[:, cs] = jnp.where(first_head, prev, o).astype(o_ref.dtype)
        prev, wins = o, nxt


def _attn_kernel(q_ref, k0_ref, k1_ref, k2_ref, v0_ref, v1_ref, v2_ref, bias_ref, o_ref):
    g = pl.program_id(0)
    args = (q_ref, (k0_ref, k1_ref, k2_ref), (v0_ref, v1_ref, v2_ref), bias_ref, o_ref)

    @pl.when(g < 2)
    def _():
        _attn_pairs(*args, first_frame=(g - 2) * ATT_ROWS)

    @pl.when(g >= 2)
    def _():
        _attn_pairs(*args, first_frame=None)


def _attention(proj, bias_full):
    T = proj.shape[0]
    n = T // ATT_ROWS

    def kv_spec(back, colblk):
        return pl.BlockSpec((ATT_ROWS, ATT_WIDTH), lambda g: (jnp.maximum(g - back, 0), colblk))

    return pl.pallas_call(
        _attn_kernel,
        grid=(n,),
        in_specs=[
            pl.BlockSpec((ATT_ROWS, ATT_WIDTH), lambda g: (g, 0)),
            kv_spec(2, 1), kv_spec(1, 1), kv_spec(0, 1),
            kv_spec(2, 2), kv_spec(1, 2), kv_spec(0, 2),
            pl.BlockSpec((ATT_HEADS, ATT_ROWS // 2, ATT_WIN), lambda g: (0, 0, 0)),
        ],
        out_specs=pl.BlockSpec((ATT_ROWS, ATT_WIDTH), lambda g: (g, 0)),
        out_shape=jax.ShapeDtypeStruct((T, ATT_WIDTH), BF16),
        compiler_params=_cparams(("arbitrary",)),
        name="band_attention",
    )(proj, proj, proj, proj, proj, proj, proj, bias_full)


def _attention_bias(table):
    rows, off, md = ATT_ROWS // 2, 2 * ATT_ROWS, MAX_REL_DIST
    n_diag = rows + ATT_WIN - 1
    n_clipped = off - md + rows
    lowest = off + md - (ATT_WIN - 1)
    assert 0 <= lowest and n_diag - n_clipped == 2 * md - lowest
    H = table.shape[0]
    v = jnp.concatenate([
        jnp.broadcast_to(table[:, 2 * md:], (H, n_clipped)),
        table[:, lowest:2 * md][:, ::-1],
    ], axis=1)
    flat = jnp.tile(v, (1, rows))
    bias = flat[:, rows - 1:rows - 1 + rows * (n_diag - 1)]
    bias = bias.reshape(H, rows, n_diag - 1)[:, :, :ATT_WIN]
    qc = np.arange(rows)[:, None] // CHUNK
    kc = np.arange(ATT_WIN)[None, :] // CHUNK
    in_band = (kc >= qc) & (kc <= qc + BAND_CHUNKS - 1)
    return jnp.where(in_band[None], bias, -jnp.inf).astype(F32)


def _gla_kernel(q_ref, k_ref, v_ref, r_ref, la_ref, nw_ref, o_ref, s_ref):
    @pl.when(pl.program_id(0) == 0)
    def _():
        s_ref[...] = jnp.zeros_like(s_ref)

    n_chunks = GLA_ROWS // CHUNK
    row = lax.broadcasted_iota(jnp.int32, (GLA_ROWS, GLA_ROWS), 0)
    col = lax.broadcasted_iota(jnp.int32, (GLA_ROWS, GLA_ROWS), 1)
    same_chunk = (row // CHUNK) == (col // CHUNK)
    causal = same_chunk & (col <= row)
    tri = jnp.where(causal, 1.0, 0.0).astype(BF16)
    blk = jnp.where(same_chunk, 1.0, 0.0).astype(BF16)

    la = la_ref[...]
    hi = la.astype(BF16)
    rem = la - hi.astype(F32)
    mid = rem.astype(BF16)
    lo = (rem - mid.astype(F32)).astype(BF16)
    b = _dot(tri, hi) + _dot(tri, mid) + _dot(tri, lo)
    b_last = _dot(blk, hi) + _dot(blk, mid) + _dot(blk, lo)

    q = q_ref[...].astype(F32) * (GLA_DK ** -0.5)
    k = k_ref[...].astype(F32)
    q_dec = (q * jnp.exp(b)).astype(BF16)
    k_inv = (k * jnp.exp(-b)).astype(BF16)
    k_end = (k * jnp.exp(b_last - b)).astype(BF16)
    nt = (((1,), (1,)), ((), ()))
    tn = (((0,), (0,)), ((), ()))

    for h in range(GLA_HEADS):
        ks = slice(h * GLA_DK, (h + 1) * GLA_DK)
        vs = slice(h * GLA_DV, (h + 1) * GLA_DV)
        v = v_ref[:, vs]
        attn = lax.dot_general(q_dec[:, ks], k_inv[:, ks], nt, preferred_element_type=F32)
        attn = jnp.where(causal, attn, 0.0).astype(BF16)
        o_intra = _dot(attn, v)

        state = s_ref[h]
        inter = []
        for c in range(n_chunks):
            rs = slice(c * CHUNK, (c + 1) * CHUNK)
            inter.append(_dot(q_dec[rs, ks], state.astype(BF16)))
            upd = lax.dot_general(k_end[rs, ks], v[rs], tn, preferred_element_type=F32)
            tot = jnp.broadcast_to(b_last[c * CHUNK:c * CHUNK + 1, ks], (GLA_DK, GLA_DK))
            decay = jnp.exp(tot.T)
            state = jnp.concatenate([decay, decay], axis=1) * state + upd
        s_ref[h] = state

        o = o_intra + jnp.concatenate(inter, axis=0)
        o = _rms_norm(o, nw_ref[...])
        r = r_ref[:, vs].astype(F32)
        o_ref[:, vs] = (o * (r * jax.nn.sigmoid(r))).astype(o_ref.dtype)


def _gla(proj, log_a, norm_w):
    T = proj.shape[0]
    return pl.pallas_call(
        _gla_kernel,
        grid=(T // GLA_ROWS,),
        in_specs=[
            pl.BlockSpec((GLA_ROWS, GLA_KEY_WIDTH), lambda g: (g, 3 * ATT_WIDTH // GLA_KEY_WIDTH)),
            pl.BlockSpec((GLA_ROWS, GLA_KEY_WIDTH), lambda g: (g, 3 * ATT_WIDTH // GLA_KEY_WIDTH + 1)),
            pl.BlockSpec((GLA_ROWS, GLA_WIDTH), lambda g: (g, 4)),
            pl.BlockSpec((GLA_ROWS, GLA_WIDTH), lambda g: (g, 5)),
            pl.BlockSpec((GLA_ROWS, GLA_KEY_WIDTH), lambda g: (g, 0)),
            pl.BlockSpec((1, GLA_DV), lambda g: (0, 0)),
        ],
        out_specs=pl.BlockSpec((GLA_ROWS, GLA_WIDTH), lambda g: (g, 0)),
        out_shape=jax.ShapeDtypeStruct((T, GLA_WIDTH), BF16),
        scratch_shapes=[pltpu.VMEM((GLA_HEADS, GLA_DK, GLA_DV), F32)],
        compiler_params=_cparams(("arbitrary",)),
        name="gla",
    )(proj, proj, proj, proj, log_a, norm_w)


def _outproj_kernel(x_ref, ya_ref, yg_ref, wa_ref, wg_ref, o_ref):
    o_ref[...] = x_ref[...] + _dot(ya_ref[...], wa_ref[...]) + _dot(yg_ref[...], wg_ref[...])


def _outproj(x, y_a, y_g, w_out):
    T = x.shape[0]
    return pl.pallas_call(
        _outproj_kernel,
        grid=(T // TM_ROW,),
        in_specs=[
            pl.BlockSpec((TM_ROW, D_MODEL), lambda i: (i, 0)),
            pl.BlockSpec((TM_ROW, ATT_WIDTH), lambda i: (i, 0)),
            pl.BlockSpec((TM_ROW, GLA_WIDTH), lambda i: (i, 0)),
            pl.BlockSpec((ATT_WIDTH, D_MODEL), lambda i: (0, 0)),
            pl.BlockSpec((GLA_WIDTH, D_MODEL), lambda i: (1, 0)),
        ],
        out_specs=pl.BlockSpec((TM_ROW, D_MODEL), lambda i: (i, 0)),
        out_shape=jax.ShapeDtypeStruct((T, D_MODEL), F32),
        compiler_params=_cparams(("arbitrary",)),
        name="outproj",
    )(x, y_a, y_g, w_out, w_out)


def _gelu_tanh(x):
    c = np.sqrt(2.0 / np.pi).astype(np.float32)
    return 0.5 * x * (1.0 + jnp.tanh(c * (x + 0.044715 * (x * x * x))))


def _ffn_kernel(x_ref, xh_ref, nw_ref, w_ref, cw_ref, cb_ref, cwp_ref, cbp_ref, wd_a, wd_b,
                o_ref, h_ref, ua_ref, ub_ref):
    i = pl.program_id(0)
    j = pl.program_id(1)
    nf = pl.num_programs(1) - 1
    piece = TM_FFN // FFN_PIECES
    half_a = slice(0, 2 * SUB_FFN)
    half_b = slice(2 * SUB_FFN, 4 * SUB_FFN)

    def up(q, cols, dst_ref):
        rs = pl.ds(0, HALO + piece) if q == 0 else pl.ds(HALO + q * piece, piece)
        dst_ref[rs, :] = _dot(h_ref[rs, :], _unpack_rows(w_ref[:, cols]))

    def act(q, src_ref, cw, cb, cols):
        r0 = HALO + q * piece
        c = (cb[:, cols]
             + src_ref[pl.ds(r0 - 2, piece), :] * cw[0:1, cols]
             + src_ref[pl.ds(r0 - 1, piece), :] * cw[1:2, cols]
             + src_ref[pl.ds(r0, piece), :] * cw[2:3, cols])
        return (_gelu_tanh(c[:, :SUB_FFN]) * c[:, SUB_FFN:]).astype(BF16)

    def down(q, a, wd):
        rs = pl.ds(q * piece, piece)
        o_ref[rs, :] += _dot(a, _unpack_rows(wd[...]))

    def run(stages):
        for s in stages[:FFN_AHEAD]:
            if s[0] is not None:
                s[0]()
        for k, (_, act_fn, wd) in enumerate(stages):
            if k + FFN_AHEAD < len(stages) and stages[k + FFN_AHEAD][0] is not None:
                stages[k + FFN_AHEAD][0]()
            if act_fn is not None:
                down(k % FFN_PIECES, act_fn(), wd)

    pieces = range(FFN_PIECES)
    ups_a = [functools.partial(up, q, half_a, ua_ref) for q in pieces]
    ups_b = [functools.partial(up, q, half_b, ub_ref) for q in pieces]
    acts_a = [functools.partial(act, q, ua_ref, cw_ref, cb_ref, half_a) for q in pieces]
    acts_pb = [functools.partial(act, q, ub_ref, cwp_ref, cbp_ref, half_b) for q in pieces]
    prev_b = [(ups_a[q], acts_pb[q], wd_b) for q in pieces]
    this_a = [(ups_b[q], acts_a[q], wd_a) for q in pieces]

    @pl.when(j == 0)
    def _():
        x = x_ref[...]
        h_ref[HALO:, :] = _rms_norm(x, nw_ref[...]).astype(BF16)
        halo = _rms_norm(xh_ref[...], nw_ref[...])
        h_ref[:HALO, :] = jnp.where(i > 0, halo, 0.0).astype(BF16)
        o_ref[...] = x
        run([(ups_a[q], None, None) for q in pieces] + this_a)

    @pl.when((j > 0) & (j < nf))
    def _():
        run(prev_b + this_a)

    @pl.when(j == nf)
    def _():
        run([(None, acts_pb[q], wd_b) for q in pieces])


def _ffn(x, nw, w_up, w_conv, b_conv, w_down):
    T = x.shape[0]
    nf = D_FF // TF_FFN
    halo_blocks = TM_FFN // HALO

    def cur(j):
        return jnp.minimum(j, nf - 1)

    def prev(j):
        return jnp.maximum(j - 1, 0)

    return pl.pallas_call(
        _ffn_kernel,
        grid=(T // TM_FFN, nf + 1),
        in_specs=[
            pl.BlockSpec((TM_FFN, D_MODEL), lambda i, j: (i, 0)),
            pl.BlockSpec((HALO, D_MODEL), lambda i, j: (jnp.maximum(i * halo_blocks - 1, 0), 0)),
            pl.BlockSpec((1, D_MODEL), lambda i, j: (0, 0)),
            pl.BlockSpec((D_MODEL // 2, 2 * TF_FFN), lambda i, j: (0, cur(j))),
            pl.BlockSpec((CONV_WIDTH, 2 * TF_FFN), lambda i, j: (0, cur(j))),
            pl.BlockSpec((1, 2 * TF_FFN), lambda i, j: (0, cur(j))),
            pl.BlockSpec((CONV_WIDTH, 2 * TF_FFN), lambda i, j: (0, prev(j))),
            pl.BlockSpec((1, 2 * TF_FFN), lambda i, j: (0, prev(j))),
            pl.BlockSpec((SUB_FFN // 2, D_MODEL), lambda i, j: (2 * cur(j), 0)),
            pl.BlockSpec((SUB_FFN // 2, D_MODEL), lambda i, j: (2 * prev(j) + 1, 0)),
        ],
        out_specs=pl.BlockSpec((TM_FFN, D_MODEL), lambda i, j: (i, 0)),
        out_shape=jax.ShapeDtypeStruct((T, D_MODEL), F32),
        scratch_shapes=[pltpu.VMEM((TM_FFN + HALO, D_MODEL), BF16),
                        pltpu.VMEM((TM_FFN + HALO, 2 * SUB_FFN), F32),
                        pltpu.VMEM((TM_FFN + HALO, 2 * SUB_FFN), F32)],
        compiler_params=_cparams(("arbitrary", "arbitrary")),
        name="conv_ffn",
    )(x, x, nw, w_up, w_conv, b_conv, w_conv, b_conv, w_down, w_down)


def _interleave_gate_val(w):
    lead = w.shape[:-1]
    w = w.reshape(*lead, 2, D_FF // SUB_FFN, SUB_FFN)
    return jnp.swapaxes(w, -3, -2).reshape(*lead, 2 * D_FF)


def _ple_kernel(x_ref, p_ref, nw_ref, wg_ref, wp_ref, fw_ref, o_ref):
    x = x_ref[...]
    h = _rms_norm(x, nw_ref[...]).astype(BF16)
    gate = jax.nn.sigmoid(_dot(h, wg_ref[...]))
    emb = _dot(p_ref[...].astype(BF16), wp_ref[...])
    o_ref[...] = _rms_norm(x + gate * emb, fw_ref[...])


def _ple(x, p, nw, w_gate, w_proj, final_w):
    T = x.shape[0]
    return pl.pallas_call(
        _ple_kernel,
        grid=(T // TM_ROW,),
        in_specs=[
            pl.BlockSpec((TM_ROW, D_MODEL), lambda i: (i, 0)),
            pl.BlockSpec((TM_ROW, PLE_DIM), lambda i: (i, 0)),
            pl.BlockSpec((1, D_MODEL), lambda i: (0, 0)),
            pl.BlockSpec((D_MODEL, D_MODEL), lambda i: (0, 0)),
            pl.BlockSpec((PLE_DIM, D_MODEL), lambda i: (0, 0)),
            pl.BlockSpec((1, D_MODEL), lambda i: (0, 0)),
        ],
        out_specs=pl.BlockSpec((TM_ROW, D_MODEL), lambda i: (i, 0)),
        out_shape=jax.ShapeDtypeStruct((T, D_MODEL), F32),
        compiler_params=_cparams(("arbitrary",)),
        name="ple_final",
    )(x, p, nw, w_gate, w_proj, final_w)


def kernel(x, p, norm_mix_w, w_in, att_rel_bias, w_gla_gate_up, b_gla_gate, gla_norm_w, w_out, norm_ffn_w, w_ffn_up, w_ffn_conv, b_ffn_conv, w_ffn_down, norm_ple_w, w_ple_gate, w_ple_proj, final_norm_w):
    B, T, D = x.shape
    assert (B, T, D) == (1, SEQ, D_MODEL) and w_in.shape[0] == 1
    xs = x[0]
    i = 0
    w_main = w_in[i, :, :MAIN_PROJ_WIDTH].astype(BF16)
    w_lr = jnp.pad(w_in[i, :, MAIN_PROJ_WIDTH:], ((0, 0), (0, LANES - GLA_GATE_RANK))).astype(BF16)
    w_up = jnp.pad(w_gla_gate_up[i], ((0, LANES - GLA_GATE_RANK), (0, 0))).astype(BF16)

    proj, log_a = _inproj(xs, norm_mix_w[i][None], w_main, w_lr, w_up, b_gla_gate[i][None])
    y_a = _attention(proj, _attention_bias(att_rel_bias[i]))
    y_g = _gla(proj, log_a, gla_norm_w[i][None])
    xs = _outproj(xs, y_a, y_g, w_out[i].astype(BF16))
    xs = _ffn(xs, norm_ffn_w[i][None], _pack_rows(_interleave_gate_val(w_ffn_up[i])),
              _interleave_gate_val(w_ffn_conv[i]), _interleave_gate_val(b_ffn_conv[i][None]),
              _pack_rows(w_ffn_down[i]))
    xs = _ple(xs, p[i, 0], norm_ple_w[i][None], w_ple_gate[i].astype(BF16), w_ple_proj[i].astype(BF16),
              final_norm_w[None])
    return xs[None]
```

```python
import functools

import jax
import jax.numpy as jnp
import numpy as np
from jax import lax
from jax.experimental import pallas as pl
from jax.experimental.pallas import tpu as pltpu

F32 = jnp.float32
BF16 = jnp.bfloat16

D_MODEL = 2048
SEQ = 16384
CHUNK = 64
PLE_DIM = 256
EPS = 1e-6

ATT_HEAD_DIM = 64
ATT_WIDTH = 1024
ATT_HEADS = 16
BAND_CHUNKS = 9
MAX_REL_DIST = 256

GLA_HEADS = 4
GLA_WIDTH = 1024
GLA_DV = 256
GLA_KEY_WIDTH = 512
GLA_DK = 128
GLA_GATE_RANK = 16
GLA_GATE_TAU = 16.0

MAIN_PROJ_WIDTH = 3 * ATT_WIDTH + 2 * GLA_KEY_WIDTH + 2 * GLA_WIDTH
D_FF = 5632
CONV_WIDTH = 3

LANES = 128
BF16_SUBLANES = 16
VMEM_LIMIT = 56 * 1024 * 1024

TM_PROJ = 1024
TN_PROJ = 1024
ATT_ROWS = 256
ATT_WIN = ATT_ROWS // 2 + (BAND_CHUNKS - 1) * CHUNK
GLA_ROWS = 256
TM_FFN = 512
TF_FFN = 512
SUB_FFN = TF_FFN // 2
FFN_PIECES = 4
FFN_AHEAD = 2
HALO = BF16_SUBLANES
TM_ROW = 512


def _rms_norm(x, w):
    ms = jnp.mean(x * x, axis=-1, keepdims=True)
    return x * lax.rsqrt(ms + EPS) * w


def _dot(a, b):
    return jnp.dot(a, b, preferred_element_type=F32)


def _pack_rows(w):
    k, n = w.shape
    pairs = jnp.swapaxes(w.astype(BF16).reshape(k // 2, 2, n), 1, 2)
    return lax.bitcast_convert_type(pairs, jnp.uint32)


def _unpack_rows(w):
    return pltpu.bitcast(w, BF16)


def _cparams(sem):
    return pltpu.CompilerParams(dimension_semantics=sem, vmem_limit_bytes=VMEM_LIMIT)


def _inproj_kernel(x_ref, nw_ref, w_ref, wlr_ref, wup_ref, bg_ref, proj_ref, loga_ref, h_ref):
    @pl.when(pl.program_id(1) == 0)
    def _():
        h = _rms_norm(x_ref[...], nw_ref[...]).astype(BF16)
        h_ref[...] = h
        gate_lr = _dot(h, wlr_ref[...]).astype(BF16)
        logit = _dot(gate_lr, wup_ref[...]) + bg_ref[...]
        log_sig = jnp.minimum(logit, 0.0) - jnp.log(1.0 + jnp.exp(-jnp.abs(logit)))
        loga_ref[...] = log_sig * (1.0 / GLA_GATE_TAU)

    proj_ref[...] = _dot(h_ref[...], w_ref[...]).astype(BF16)


def _inproj(x, nw, w_main, w_lr, w_up, b_gate):
    T = x.shape[0]
    grid = (T // TM_PROJ, MAIN_PROJ_WIDTH // TN_PROJ)
    return pl.pallas_call(
        _inproj_kernel,
        grid=grid,
        in_specs=[
            pl.BlockSpec((TM_PROJ, D_MODEL), lambda i, j: (i, 0)),
            pl.BlockSpec((1, D_MODEL), lambda i, j: (0, 0)),
            pl.BlockSpec((D_MODEL, TN_PROJ), lambda i, j: (0, j)),
            pl.BlockSpec((D_MODEL, LANES), lambda i, j: (0, 0)),
            pl.BlockSpec((LANES, GLA_KEY_WIDTH), lambda i, j: (0, 0)),
            pl.BlockSpec((1, GLA_KEY_WIDTH), lambda i, j: (0, 0)),
        ],
        out_specs=[
            pl.BlockSpec((TM_PROJ, TN_PROJ), lambda i, j: (i, j)),
            pl.BlockSpec((TM_PROJ, GLA_KEY_WIDTH), lambda i, j: (i, 0)),
        ],
        out_shape=[
            jax.ShapeDtypeStruct((T, MAIN_PROJ_WIDTH), BF16),
            jax.ShapeDtypeStruct((T, GLA_KEY_WIDTH), F32),
        ],
        scratch_shapes=[pltpu.VMEM((TM_PROJ, D_MODEL), BF16)],
        compiler_params=_cparams(("arbitrary", "arbitrary")),
        name="inproj",
    )(x, nw, w_main, w_lr, w_up, b_gate)


def _attn_pairs(q_ref, k_refs, v_refs, bias_ref, o_ref, first_frame):
    half = ATT_ROWS // 2
    lane = lax.broadcasted_iota(jnp.int32, (ATT_ROWS, LANES), 1)
    first_head = lane < ATT_HEAD_DIM
    first_head_v = lax.broadcasted_iota(jnp.int32, (3 * ATT_ROWS, LANES), 1) < ATT_HEAD_DIM
    nt = (((1,), (1,)), ((), ()))
    if first_frame is not None:
        col = lax.broadcasted_iota(jnp.int32, (half, ATT_WIN), 1)
        valid = (col >= -first_frame, col >= -first_frame - half)
    scale = jnp.asarray(ATT_HEAD_DIM ** -0.5, BF16)
    zero_blk = jnp.zeros((half, half), BF16)

    def scores(hd):
        cs = slice(hd // 2 * LANES, (hd // 2 + 1) * LANES)
        q = q_ref[:, cs] * scale
        zero = jnp.zeros_like(q)
        qh = jnp.where(first_head, q, zero) if hd % 2 == 0 else jnp.where(first_head, zero, q)
        s0, s1, s2 = [lax.dot_general(qh, r[:, cs], nt, preferred_element_type=F32) for r in k_refs]
        return (jnp.concatenate([s0[:half], s1[:half], s2[:half, :half]], axis=1),
                jnp.concatenate([s0[half:, half:], s1[half:], s2[half:]], axis=1))

    def weights(wins, hd):
        es = []
        for w in range(2):
            s = wins[w] + bias_ref[hd]
            if first_frame is not None:
                s = jnp.where(valid[w], s, -jnp.inf)
            m = jnp.max(s, axis=-1, keepdims=True)
            es.append(jnp.exp(s - m).astype(BF16))
        return jnp.concatenate([jnp.concatenate([es[0], zero_blk], axis=1),
                                jnp.concatenate([zero_blk, es[1]], axis=1)], axis=0)

    def values(e, hd):
        cs = slice(hd // 2 * LANES, (hd // 2 + 1) * LANES)
        v = jnp.concatenate([r[:, cs] for r in v_refs], axis=0)
        one = jnp.ones_like(v)
        vh = jnp.where(first_head_v, v, one) if hd % 2 == 0 else jnp.where(first_head_v, one, v)
        o = _dot(e, vh)
        return o / pltpu.roll(o, ATT_HEAD_DIM, 1)

    wins = scores(0)
    prev = None
    for hd in range(ATT_HEADS):
        nxt = scores(hd + 1) if hd + 1 < ATT_HEADS else None
        o = values(weights(wins, hd), hd)
        if hd % 2 == 1:
            cs = slice(hd // 2 * LANES, (hd // 2 + 1) * LANES)
            o_ref[:, cs] = jnp.where(first_head, prev, o).astype(o_ref.dtype)
        prev, wins = o, nxt


def _attn_kernel(q_ref, k0_ref, k1_ref, k2_ref, v0_ref, v1_ref, v2_ref, bias_ref, o_ref):
    g = pl.program_id(0)
    args = (q_ref, (k0_ref, k1_ref, k2_ref), (v0_ref, v1_ref, v2_ref), bias_ref, o_ref)

    @pl.when(g < 2)
    def _():
        _attn_pairs(*args, first_frame=(g - 2) * ATT_ROWS)

    @pl.when(g >= 2)
    def _():
        _attn_pairs(*args, first_frame=None)


def _attention(proj, bias_full):
    T = proj.shape[0]
    n = T // ATT_ROWS

    def kv_spec(back, colblk):
        return pl.BlockSpec((ATT_ROWS, ATT_WIDTH), lambda g: (jnp.maximum(g - back, 0), colblk))

    return pl.pallas_call(
        _attn_kernel,
        grid=(n,),
        in_specs=[
            pl.BlockSpec((ATT_ROWS, ATT_WIDTH), lambda g: (g, 0)),
            kv_spec(2, 1), kv_spec(1, 1), kv_spec(0, 1),
            kv_spec(2, 2), kv_spec(1, 2), kv_spec(0, 2),
            pl.BlockSpec((ATT_HEADS, ATT_ROWS // 2, ATT_WIN), lambda g: (0, 0, 0)),
        ],
        out_specs=pl.BlockSpec((ATT_ROWS, ATT_WIDTH), lambda g: (g, 0)),
        out_shape=jax.ShapeDtypeStruct((T, ATT_WIDTH), BF16),
        compiler_params=_cparams(("arbitrary",)),
        name="band_attention",
    )(proj, proj, proj, proj, proj, proj, proj, bias_full)


def _attention_bias(table):
    rows, off, md = ATT_ROWS // 2, 2 * ATT_ROWS, MAX_REL_DIST
    n_diag = rows + ATT_WIN - 1
    n_clipped = off - md + rows
    lowest = off + md - (ATT_WIN - 1)
    assert 0 <= lowest and n_diag - n_clipped == 2 * md - lowest
    H = table.shape[0]
    v = jnp.concatenate([
        jnp.broadcast_to(table[:, 2 * md:], (H, n_clipped)),
        table[:, lowest:2 * md][:, ::-1],
    ], axis=1)
    flat = jnp.tile(v, (1, rows))
    bias = flat[:, rows - 1:rows - 1 + rows * (n_diag - 1)]
    bias = bias.reshape(H, rows, n_diag - 1)[:, :, :ATT_WIN]
    qc = np.arange(rows)[:, None] // CHUNK
    kc = np.arange(ATT_WIN)[None, :] // CHUNK
    in_band = (kc >= qc) & (kc <= qc + BAND_CHUNKS - 1)
    return jnp.where(in_band[None], bias, -jnp.inf).astype(F32)


def _gla_kernel(q_ref, k_ref, v_ref, r_ref, la_ref, nw_ref, o_ref, s_ref):
    @pl.when(pl.program_id(0) == 0)
    def _():
        s_ref[...] = jnp.zeros_like(s_ref)

    n_chunks = GLA_ROWS // CHUNK
    row = lax.broadcasted_iota(jnp.int32, (GLA_ROWS, GLA_ROWS), 0)
    col = lax.broadcasted_iota(jnp.int32, (GLA_ROWS, GLA_ROWS), 1)
    same_chunk = (row // CHUNK) == (col // CHUNK)
    causal = same_chunk & (col <= row)
    tri = jnp.where(causal, 1.0, 0.0).astype(BF16)
    blk = jnp.where(same_chunk, 1.0, 0.0).astype(BF16)

    la = la_ref[...]
    hi = la.astype(BF16)
    rem = la - hi.astype(F32)
    mid = rem.astype(BF16)
    lo = (rem - mid.astype(F32)).astype(BF16)
    b = _dot(tri, hi) + _dot(tri, mid) + _dot(tri, lo)
    b_last = _dot(blk, hi) + _dot(blk, mid) + _dot(blk, lo)

    q = q_ref[...].astype(F32) * (GLA_DK ** -0.5)
    k = k_ref[...].astype(F32)
    q_dec = (q * jnp.exp(b)).astype(BF16)
    k_inv = (k * jnp.exp(-b)).astype(BF16)
    k_end = (k * jnp.exp(b_last - b)).astype(BF16)
    nt = (((1,), (1,)), ((), ()))
    tn = (((0,), (0,)), ((), ()))

    for h in range(GLA_HEADS):
        ks = slice(h * GLA_DK, (h + 1) * GLA_DK)
        vs = slice(h * GLA_DV, (h + 1) * GLA_DV)
        v = v_ref[:, vs]
        attn = lax.dot_general(q_dec[:, ks], k_inv[:, ks], nt, preferred_element_type=F32)
        attn = jnp.where(causal, attn, 0.0).astype(BF16)
        o_intra = _dot(attn, v)

        state = s_ref[h]
        inter = []
        for c in range(n_chunks):
            rs = slice(c * CHUNK, (c + 1) * CHUNK)
            inter.append(_dot(q_dec[rs, ks], state.astype(BF16)))
            upd = lax.dot_general(k_end[rs, ks], v[rs], tn, preferred_element_type=F32)
            tot = jnp.broadcast_to(b_last[c * CHUNK:c * CHUNK + 1, ks], (GLA_DK, GLA_DK))
            decay = jnp.exp(tot.T)
            state = jnp.concatenate([decay, decay], axis=1) * state + upd
        s_ref[h] = state

        o = o_intra + jnp.concatenate(inter, axis=0)
        o = _rms_norm(o, nw_ref[...])
        r = r_ref[:, vs].astype(F32)
        o_ref[:, vs] = (o * (r * jax.nn.sigmoid(r))).astype(o_ref.dtype)


def _gla(proj, log_a, norm_w):
    T = proj.shape[0]
    return pl.pallas_call(
        _gla_kernel,
        grid=(T // GLA_ROWS,),
        in_specs=[
            pl.BlockSpec((GLA_ROWS, GLA_KEY_WIDTH), lambda g: (g, 3 * ATT_WIDTH // GLA_KEY_WIDTH)),
            pl.BlockSpec((GLA_ROWS, GLA_KEY_WIDTH), lambda g: (g, 3 * ATT_WIDTH // GLA_KEY_WIDTH + 1)),
            pl.BlockSpec((GLA_ROWS, GLA_WIDTH), lambda g: (g, 4)),
            pl.BlockSpec((GLA_ROWS, GLA_WIDTH), lambda g: (g, 5)),
            pl.BlockSpec((GLA_ROWS, GLA_KEY_WIDTH), lambda g: (g, 0)),
            pl.BlockSpec((1, GLA_DV), lambda g: (0, 0)),
        ],
        out_specs=pl.BlockSpec((GLA_ROWS, GLA_WIDTH), lambda g: (g, 0)),
        out_shape=jax.ShapeDtypeStruct((T, GLA_WIDTH), BF16),
        scratch_shapes=[pltpu.VMEM((GLA_HEADS, GLA_DK, GLA_DV), F32)],
        compiler_params=_cparams(("arbitrary",)),
        name="gla",
    )(proj, proj, proj, proj, log_a, norm_w)


def _outproj_kernel(x_ref, ya_ref, yg_ref, wa_ref, wg_ref, o_ref):
    o_ref[...] = x_ref[...] + _dot(ya_ref[...], wa_ref[...]) + _dot(yg_ref[...], wg_ref[...])


def _outproj(x, y_a, y_g, w_out):
    T = x.shape[0]
    return pl.pallas_call(
        _outproj_kernel,
        grid=(T // TM_ROW,),
        in_specs=[
            pl.BlockSpec((TM_ROW, D_MODEL), lambda i: (i, 0)),
            pl.BlockSpec((TM_ROW, ATT_WIDTH), lambda i: (i, 0)),
            pl.BlockSpec((TM_ROW, GLA_WIDTH), lambda i: (i, 0)),
            pl.BlockSpec((ATT_WIDTH, D_MODEL), lambda i: (0, 0)),
            pl.BlockSpec((GLA_WIDTH, D_MODEL), lambda i: (1, 0)),
        ],
        out_specs=pl.BlockSpec((TM_ROW, D_MODEL), lambda i: (i, 0)),
        out_shape=jax.ShapeDtypeStruct((T, D_MODEL), F32),
        compiler_params=_cparams(("arbitrary",)),
        name="outproj",
    )(x, y_a, y_g, w_out, w_out)


def _gelu_tanh(x):
    c = np.sqrt(2.0 / np.pi).astype(np.float32)
    return 0.5 * x * (1.0 + jnp.tanh(c * (x + 0.044715 * (x * x * x))))


def _ffn_kernel(x_ref, xh_ref, nw_ref, w_ref, cw_ref, cb_ref, cwp_ref, cbp_ref, wd_a, wd_b,
                o_ref, h_ref, ua_ref, ub_ref):
    i = pl.program_id(0)
    j = pl.program_id(1)
    nf = pl.num_programs(1) - 1
    piece = TM_FFN // FFN_PIECES
    half_a = slice(0, 2 * SUB_FFN)
    half_b = slice(2 * SUB_FFN, 4 * SUB_FFN)

    def up(q, cols, dst_ref):
        rs = pl.ds(0, HALO + piece) if q == 0 else pl.ds(HALO + q * piece, piece)
        dst_ref[rs, :] = _dot(h_ref[rs, :], w_ref[:, cols])

    def act(q, src_ref, cw, cb, cols):
        r0 = HALO + q * piece
        c = (cb[:, cols]
             + src_ref[pl.ds(r0 - 2, piece), :] * cw[0:1, cols]
             + src_ref[pl.ds(r0 - 1, piece), :] * cw[1:2, cols]
             + src_ref[pl.ds(r0, piece), :] * cw[2:3, cols])
        return (_gelu_tanh(c[:, :SUB_FFN]) * c[:, SUB_FFN:]).astype(BF16)

    def down(q, a, wd):
        rs = pl.ds(q * piece, piece)
        o_ref[rs, :] += _dot(a, wd[...])

    def run(stages):
        for s in stages[:FFN_AHEAD]:
            if s[0] is not None:
                s[0]()
        for k, (_, act_fn, wd) in enumerate(stages):
            if k + FFN_AHEAD < len(stages) and stages[k + FFN_AHEAD][0] is not None:
                stages[k + FFN_AHEAD][0]()
            if act_fn is not None:
                down(k % FFN_PIECES, act_fn(), wd)

    pieces = range(FFN_PIECES)
    ups_a = [functools.partial(up, q, half_a, ua_ref) for q in pieces]
    ups_b = [functools.partial(up, q, half_b, ub_ref) for q in pieces]
    acts_a = [functools.partial(act, q, ua_ref, cw_ref, cb_ref, half_a) for q in pieces]
    acts_pb = [functools.partial(act, q, ub_ref, cwp_ref, cbp_ref, half_b) for q in pieces]
    prev_b = [(ups_a[q], acts_pb[q], wd_b) for q in pieces]
    this_a = [(ups_b[q], acts_a[q], wd_a) for q in pieces]

    @pl.when(j == 0)
    def _():
        x = x_ref[...]
        h_ref[HALO:, :] = _rms_norm(x, nw_ref[...]).astype(BF16)
        halo = _rms_norm(xh_ref[...], nw_ref[...])
        h_ref[:HALO, :] = jnp.where(i > 0, halo, 0.0).astype(BF16)
        o_ref[...] = x
        run([(ups_a[q], None, None) for q in pieces] + this_a)

    @pl.when((j > 0) & (j < nf))
    def _():
        run(prev_b + this_a)

    @pl.when(j == nf)
    def _():
        run([(None, acts_pb[q], wd_b) for q in pieces])


def _ffn(x, nw, w_up, w_conv, b_conv, w_down):
    T = x.shape[0]
    nf = D_FF // TF_FFN
    halo_blocks = TM_FFN // HALO

    def cur(j):
        return jnp.minimum(j, nf - 1)

    def prev(j):
        return jnp.maximum(j - 1, 0)

    return pl.pallas_call(
        _ffn_kernel,
        grid=(T // TM_FFN, nf + 1),
        in_specs=[
            pl.BlockSpec((TM_FFN, D_MODEL), lambda i, j: (i, 0)),
            pl.BlockSpec((HALO, D_MODEL), lambda i, j: (jnp.maximum(i * halo_blocks - 1, 0), 0)),
            pl.BlockSpec((1, D_MODEL), lambda i, j: (0, 0)),
            pl.BlockSpec((D_MODEL, 2 * TF_FFN), lambda i, j: (0, cur(j))),
            pl.BlockSpec((CONV_WIDTH, 2 * TF_FFN), lambda i, j: (0, cur(j))),
            pl.BlockSpec((1, 2 * TF_FFN), lambda i, j: (0, cur(j))),
            pl.BlockSpec((CONV_WIDTH, 2 * TF_FFN), lambda i, j: (0, prev(j))),
            pl.BlockSpec((1, 2 * TF_FFN), lambda i, j: (0, prev(j))),
            pl.BlockSpec((SUB_FFN, D_MODEL), lambda i, j: (2 * cur(j), 0)),
            pl.BlockSpec((SUB_FFN, D_MODEL), lambda i, j: (2 * prev(j) + 1, 0)),
        ],
        out_specs=pl.BlockSpec((TM_FFN, D_MODEL), lambda i, j: (i, 0)),
        out_shape=jax.ShapeDtypeStruct((T, D_MODEL), F32),
        scratch_shapes=[pltpu.VMEM((TM_FFN + HALO, D_MODEL), BF16),
                        pltpu.VMEM((TM_FFN + HALO, 2 * SUB_FFN), F32),
                        pltpu.VMEM((TM_FFN + HALO, 2 * SUB_FFN), F32)],
        compiler_params=_cparams(("arbitrary", "arbitrary")),
        name="conv_ffn",
    )(x, x, nw, w_up, w_conv, b_conv, w_conv, b_conv, w_down, w_down)


def _interleave_gate_val(w):
    lead = w.shape[:-1]
    w = w.reshape(*lead, 2, D_FF // SUB_FFN, SUB_FFN)
    return jnp.swapaxes(w, -3, -2).reshape(*lead, 2 * D_FF)


def _ple_kernel(x_ref, p_ref, nw_ref, wg_ref, wp_ref, fw_ref, o_ref):
    x = x_ref[...]
    h = _rms_norm(x, nw_ref[...]).astype(BF16)
    gate = jax.nn.sigmoid(_dot(h, wg_ref[...]))
    emb = _dot(p_ref[...].astype(BF16), wp_ref[...])
    o_ref[...] = _rms_norm(x + gate * emb, fw_ref[...])


def _ple(x, p, nw, w_gate, w_proj, final_w):
    T = x.shape[0]
    return pl.pallas_call(
        _ple_kernel,
        grid=(T // TM_ROW,),
        in_specs=[
            pl.BlockSpec((TM_ROW, D_MODEL), lambda i: (i, 0)),
            pl.BlockSpec((TM_ROW, PLE_DIM), lambda i: (i, 0)),
            pl.BlockSpec((1, D_MODEL), lambda i: (0, 0)),
            pl.BlockSpec((D_MODEL, D_MODEL), lambda i: (0, 0)),
            pl.BlockSpec((PLE_DIM, D_MODEL), lambda i: (0, 0)),
            pl.BlockSpec((1, D_MODEL), lambda i: (0, 0)),
        ],
        out_specs=pl.BlockSpec((TM_ROW, D_MODEL), lambda i: (i, 0)),
        out_shape=jax.ShapeDtypeStruct((T, D_MODEL), F32),
        compiler_params=_cparams(("arbitrary",)),
        name="ple_final",
    )(x, p, nw, w_gate, w_proj, final_w)


def kernel(x, p, norm_mix_w, w_in, att_rel_bias, w_gla_gate_up, b_gla_gate, gla_norm_w, w_out, norm_ffn_w, w_ffn_up, w_ffn_conv, b_ffn_conv, w_ffn_down, norm_ple_w, w_ple_gate, w_ple_proj, final_norm_w):
    B, T, D = x.shape
    assert (B, T, D) == (1, SEQ, D_MODEL) and w_in.shape[0] == 1
    xs = x[0]
    i = 0
    w_main = w_in[i, :, :MAIN_PROJ_WIDTH].astype(BF16)
    w_lr = jnp.pad(w_in[i, :, MAIN_PROJ_WIDTH:], ((0, 0), (0, LANES - GLA_GATE_RANK))).astype(BF16)
    w_up = jnp.pad(w_gla_gate_up[i], ((0, LANES - GLA_GATE_RANK), (0, 0))).astype(BF16)

    proj, log_a = _inproj(xs, norm_mix_w[i][None], w_main, w_lr, w_up, b_gla_gate[i][None])
    y_a = _attention(proj, _attention_bias(att_rel_bias[i]))
    y_g = _gla(proj, log_a, gla_norm_w[i][None])
    xs = _outproj(xs, y_a, y_g, w_out[i].astype(BF16))
    xs = _ffn(xs, norm_ffn_w[i][None], _interleave_gate_val(w_ffn_up[i]).astype(BF16),
              _interleave_gate_val(w_ffn_conv[i]), _interleave_gate_val(b_ffn_conv[i][None]),
              w_ffn_down[i].astype(BF16))
    xs = _ple(xs, p[i, 0], norm_ple_w[i][None], w_ple_gate[i].astype(BF16), w_ple_proj[i].astype(BF16),
              final_norm_w[None])
    return xs[None]
```
